```python
import jax, jax.numpy as jnp
from jax import lax
import numpy as np

D_MODEL = 1024
BATCH = 8
SEQ = 8192
DEPTH = 1

N_META = 16
RET_HEADS = 4
RET_QK_DIM = D_MODEL // RET_HEADS
RET_WIDTH = 2 * D_MODEL
RET_V_DIM = RET_WIDTH // RET_HEADS
RET_CHUNK = 64
GLA_HEADS = 4
GLA_K_DIM = (D_MODEL // 2) // GLA_HEADS
GLA_WIDTH = D_MODEL
GLA_V_DIM = GLA_WIDTH // GLA_HEADS
GLA_GATE_RANK = 16
GLA_GATE_TAU = 16.0
GLA_CHUNK = 16
ROPE_BASE = 10000.0
EPS = 1e-6
IN_SIZES = (RET_HEADS * RET_QK_DIM, RET_HEADS * RET_QK_DIM, RET_WIDTH, RET_WIDTH,
            GLA_HEADS * GLA_K_DIM, GLA_HEADS * GLA_K_DIM, GLA_WIDTH, GLA_WIDTH,
            GLA_GATE_RANK, D_MODEL, D_MODEL)
IN_COLS = sum(IN_SIZES)

kernel_name = "hybrid_retention_gla_gated_merge"


def rms_norm(x, gain):
    xf = x.astype(jnp.float32)
    y = xf * lax.rsqrt(jnp.mean(xf * xf, axis=-1, keepdims=True) + EPS) * gain.astype(jnp.float32)
    return y.astype(x.dtype)


def head_group_norm(o, gain):
    of = o.astype(jnp.float32)
    mu = jnp.mean(of, axis=-1, keepdims=True)
    var = jnp.mean(jnp.square(of - mu), axis=-1, keepdims=True)
    return ((of - mu) * lax.rsqrt(var + EPS) * gain.astype(jnp.float32)).astype(o.dtype)


def head_rms_norm(o, gain):
    of = o.astype(jnp.float32)
    return (of * lax.rsqrt(jnp.mean(of * of, axis=-1, keepdims=True) + EPS) * gain.astype(jnp.float32)).astype(o.dtype)


def rope(t, pos):
    half = t.shape[-1] // 2
    inv = ROPE_BASE ** (-jnp.arange(half, dtype=jnp.float32) / half)
    ang = pos[:, None] * inv[None, :]
    cos = jnp.cos(ang)[None, :, None, :]
    sin = jnp.sin(ang)[None, :, None, :]
    t1 = t[..., :half].astype(jnp.float32)
    t2 = t[..., half:].astype(jnp.float32)
    return jnp.concatenate([t1 * cos - t2 * sin, t2 * cos + t1 * sin], axis=-1).astype(t.dtype)


def to_chunks(t, c):
    pad = (-N_META) % c
    t = jnp.pad(t, ((0, 0), (pad, 0), (0, 0), (0, 0)))
    b, lp, h, d = t.shape
    return t.reshape(b, lp // c, c, h, d)


def from_chunks(t, c):
    b, n, _, h, d = t.shape
    pad = (-N_META) % c
    return t.reshape(b, n * c, h, d)[:, pad:]


def retention_chunked(q, k, v):
    c = RET_CHUNK
    qc, kc, vc = to_chunks(q, c), to_chunks(k, c), to_chunks(v, c)
    log_gamma = jnp.log1p(-(2.0 ** (-5.0 - jnp.arange(RET_HEADS, dtype=jnp.float32))))
    idx = jnp.arange(c, dtype=jnp.float32)
    rel = idx[:, None] - idx[None, :]
    decay = jnp.where(rel[None] >= 0, jnp.exp(jnp.maximum(rel, 0.0)[None] * log_gamma[:, None, None]), 0.0)
    scores = jnp.einsum('bnihd,bnjhd->bnhij', qc, kc) * decay[None, None]
    intra = jnp.einsum('bnhij,bnjhe->bnihe', scores, vc)
    xi = jnp.exp((idx[:, None] + 1.0) * log_gamma[None, :])
    zeta = jnp.exp((c - 1.0 - idx[:, None]) * log_gamma[None, :])
    gamma_c = jnp.exp(c * log_gamma)

    def step(state, xs):
        q_n, k_n, v_n = xs
        inter = jnp.einsum('bihd,bhde->bihe', q_n, state) * xi[None, :, :, None]
        state = state * gamma_c[None, :, None, None] + jnp.einsum('bjhd,bjhe->bhde', k_n * zeta[None, :, :, None], v_n)
        return state, inter

    bsz = q.shape[0]
    state0 = jnp.zeros((bsz, RET_HEADS, RET_QK_DIM, RET_V_DIM), jnp.float32)
    xs = (jnp.moveaxis(qc, 1, 0), jnp.moveaxis(kc, 1, 0), jnp.moveaxis(vc, 1, 0))
    _, inter = lax.scan(step, state0, xs)
    out = intra + jnp.moveaxis(inter, 0, 1)
    return from_chunks(out, c).astype(v.dtype)


def gla_chunked(q, k, v, log_a):
    c = GLA_CHUNK
    qc, kc, vc = to_chunks(q, c), to_chunks(k, c), to_chunks(v, c)
    ac = to_chunks(log_a.astype(jnp.float32), c)
    b = jnp.cumsum(ac, axis=2)
    b_last = b[:, :, -1]
    q_dec = qc * jnp.exp(b)
    k_inv = kc * jnp.exp(-b)
    k_end = kc * jnp.exp(b_last[:, :, None] - b)
    mask = jnp.tril(jnp.ones((c, c), dtype=bool))
    scores = jnp.where(mask, jnp.einsum('bnihd,bnjhd->bnhij', q_dec, k_inv), 0.0)
    intra = jnp.einsum('bnhij,bnjhe->bnihe', scores, vc)

    def step(state, xs):
        q_n, k_n, v_n, a_n = xs
        inter = jnp.einsum('bihd,bhde->bihe', q_n, state)
        state = state * jnp.exp(a_n)[..., None] + jnp.einsum('bjhd,bjhe->bhde', k_n, v_n)
        return state, inter

    bsz = q.shape[0]
    state0 = jnp.zeros((bsz, GLA_HEADS, GLA_K_DIM, GLA_V_DIM), jnp.float32)
    xs = (jnp.moveaxis(q_dec, 1, 0), jnp.moveaxis(k_end, 1, 0), jnp.moveaxis(vc, 1, 0), jnp.moveaxis(b_last, 1, 0))
    _, inter = lax.scan(step, state0, xs)
    out = intra + jnp.moveaxis(inter, 0, 1)
    return from_chunks(out, c).astype(v.dtype)


def hybrid_layer(h, norm_gain, w_in, w_gate_up, b_gate, ret_norm_gain, gla_norm_gain,
                 w_branch_ret, w_branch_gla, w_out):
    bsz, length, _ = h.shape
    u = rms_norm(h, norm_gain)
    proj = u @ w_in
    points = [int(p) for p in np.cumsum(IN_SIZES)[:-1]]
    (rq, rk, rv, rg, gq, gk, gv, gg, glr, m_ret, m_gla) = jnp.split(proj, points, axis=-1)

    pos = jnp.arange(length, dtype=jnp.float32)
    rq = rope(rq.reshape(bsz, length, RET_HEADS, RET_QK_DIM), pos)
    rk = rope(rk.reshape(bsz, length, RET_HEADS, RET_QK_DIM), pos) * (RET_QK_DIM ** -0.5)
    rv = rv.reshape(bsz, length, RET_HEADS, RET_V_DIM)
    o_ret = retention_chunked(rq, rk, rv)
    o_ret = head_group_norm(o_ret, ret_norm_gain.reshape(RET_HEADS, RET_V_DIM)).reshape(bsz, length, RET_WIDTH)
    o_ret = o_ret * jax.nn.silu(rg)

    gq = gq.reshape(bsz, length, GLA_HEADS, GLA_K_DIM) * (GLA_K_DIM ** -0.5)
    gk = gk.reshape(bsz, length, GLA_HEADS, GLA_K_DIM)
    gv = gv.reshape(bsz, length, GLA_HEADS, GLA_V_DIM)
    log_a = jax.nn.log_sigmoid((glr @ w_gate_up + b_gate).astype(jnp.float32)) / GLA_GATE_TAU
    log_a = log_a.reshape(bsz, length, GLA_HEADS, GLA_K_DIM)
    o_gla = gla_chunked(gq, gk, gv, log_a)
    o_gla = head_rms_norm(o_gla, gla_norm_gain.reshape(GLA_HEADS, GLA_V_DIM)).reshape(bsz, length, GLA_WIDTH)
    o_gla = o_gla * jax.nn.silu(gg)

    merged = jax.nn.sigmoid(m_ret) * (o_ret @ w_branch_ret) + jax.nn.sigmoid(m_gla) * (o_gla @ w_branch_gla)
    return h + merged @ w_out


def setup_inputs(seed: int = 0) -> dict:
    key = jax.random.key(seed)
    ks = jax.random.split(key, 12)
    f = jnp.float32
    gk_dim = GLA_HEADS * GLA_K_DIM
    return {
        "x": jax.random.normal(ks[0], (BATCH, SEQ, D_MODEL), f),
        "meta_tokens": jax.random.normal(ks[1], (N_META, D_MODEL), f),
        "norm_gain": 1.0 + 0.02 * jax.random.normal(ks[2], (DEPTH, D_MODEL), f),
        "w_in": jax.random.normal(ks[3], (DEPTH, D_MODEL, IN_COLS), f) * D_MODEL ** -0.5,
        "w_gate_up": jax.random.normal(ks[4], (DEPTH, GLA_GATE_RANK, gk_dim), f) * GLA_GATE_RANK ** -0.5,
        "b_gate": 0.01 * jax.random.normal(ks[5], (DEPTH, gk_dim), f),
        "ret_norm_gain": 1.0 + 0.02 * jax.random.normal(ks[6], (DEPTH, RET_WIDTH), f),
        "gla_norm_gain": 1.0 + 0.02 * jax.random.normal(ks[7], (DEPTH, GLA_WIDTH), f),
        "w_branch_ret": jax.random.normal(ks[8], (DEPTH, RET_WIDTH, D_MODEL), f) * RET_WIDTH ** -0.5,
        "w_branch_gla": jax.random.normal(ks[9], (DEPTH, GLA_WIDTH, D_MODEL), f) * GLA_WIDTH ** -0.5,
        "w_out": jax.random.normal(ks[10], (DEPTH, D_MODEL, D_MODEL), f) * D_MODEL ** -0.5,
        "final_norm_gain": 1.0 + 0.02 * jax.random.normal(ks[11], (D_MODEL,), f),
    }


def reference(x, meta_tokens, norm_gain, w_in, w_gate_up, b_gate, ret_norm_gain, gla_norm_gain,
              w_branch_ret, w_branch_gla, w_out, final_norm_gain):
    bsz = x.shape[0]
    meta = jnp.broadcast_to(meta_tokens.astype(x.dtype)[None], (bsz, N_META, D_MODEL))
    h = jnp.concatenate([meta, x], axis=1)
    for layer in range(DEPTH):
        h = hybrid_layer(h, norm_gain[layer], w_in[layer], w_gate_up[layer], b_gate[layer],
                         ret_norm_gain[layer], gla_norm_gain[layer], w_branch_ret[layer],
                         w_branch_gla[layer], w_out[layer])
    h = rms_norm(h, final_norm_gain)
    return h[:, N_META:]
```

```python
import functools
import math

import numpy as np
import jax
import jax.numpy as jnp
from jax import lax
from jax.experimental import pallas as pl
from jax.experimental.pallas import tpu as pltpu

D_MODEL = 1024
N_META = 16
RET_HEADS = 4
RET_QK = 256
RET_V = 512
RET_WIDTH = RET_HEADS * RET_V
GLA_HEADS = 4
GLA_K = 128
GLA_V = 256
GLA_WIDTH = GLA_HEADS * GLA_V
GLA_RANK = 16
GLA_TAU = 16.0
ROPE_BASE = 10000.0
EPS = 1e-6

LANES = 128
BLOCK = 256
GLA_SUB = 64
META_ROWS = 128
RANK_PAD = LANES
VMEM_LIMIT_BYTES = 60 * 1024 * 1024

OFF_RQ = 0
OFF_RK = OFF_RQ + RET_HEADS * RET_QK
OFF_RV = OFF_RK + RET_HEADS * RET_QK
OFF_RG = OFF_RV + RET_WIDTH
OFF_GQ = OFF_RG + RET_WIDTH
OFF_GK = OFF_GQ + GLA_HEADS * GLA_K
OFF_GV = OFF_GK + GLA_HEADS * GLA_K
OFF_GG = OFF_GV + GLA_WIDTH
OFF_MR = OFF_GG + GLA_WIDTH
OFF_MG = OFF_MR + D_MODEL
OFF_LR = OFF_MG + D_MODEL
W_COLS = OFF_LR + RANK_PAD

RET_LOG_GAMMA = tuple(math.log1p(-(2.0 ** (-5.0 - h))) for h in range(RET_HEADS))

_F32 = jnp.float32
_BF16 = jnp.bfloat16


def _mm(a, b):
    return jnp.dot(a, b, preferred_element_type=_F32)


def _mm_nt(a, b):
    return lax.dot_general(a, b, (((1,), (1,)), ((), ())), preferred_element_type=_F32)


def _mm_tn(a, b):
    return lax.dot_general(a, b, (((0,), (0,)), ((), ())), preferred_element_type=_F32)


def _sigmoid(x):
    return 1.0 / (1.0 + jnp.exp(-x))


def _log_sigmoid(x):
    return jnp.minimum(x, 0.0) - jnp.log1p(jnp.exp(-jnp.abs(x)))


def _rms_norm(x, gain):
    return x * lax.rsqrt(jnp.mean(x * x, axis=-1, keepdims=True) + EPS) * gain


def _rope(t, cos, sin):
    t1 = t[:, :LANES]
    t2 = t[:, LANES:]
    return jnp.concatenate([t1 * cos - t2 * sin, t2 * cos + t1 * sin], axis=-1)


def _cumsum_rows(x, tri):
    hi = x.astype(_BF16)
    r1 = x - hi.astype(_F32)
    mid = r1.astype(_BF16)
    lo = (r1 - mid.astype(_F32)).astype(_BF16)
    return _mm(tri, hi) + _mm(tri, mid) + _mm(tri, lo)


def _gla_log_decay(u, w_ref, wgu_ref, bg_ref):
    glr = _mm(u, w_ref[:, OFF_LR:OFF_LR + RANK_PAD]).astype(_BF16)
    z = _mm(glr, wgu_ref[...]) + bg_ref[...]
    return _log_sigmoid(z) * (1.0 / GLA_TAU)


def _main_kernel(x_ref, cos_ref, sin_ref, ng_ref, w_ref, wgu_ref, bg_ref, rgain_ref, ggain_ref,
                 wbr_ref, wbg_ref, wout_ref, fng_ref, sret0_ref, sgla0_ref,
                 o_ref, sret_ref, sgla_ref, oret_ref, ogla_ref):
    j = pl.program_id(1)

    @pl.when(j == 0)
    def _():
        sret_ref[...] = sret0_ref[...]
        sgla_ref[...] = sgla0_ref[...]

    x = x_ref[...]
    u = _rms_norm(x, ng_ref[...]).astype(_BF16)
    cos = cos_ref[...]
    sin = sin_ref[...]

    row = lax.broadcasted_iota(jnp.int32, (BLOCK, BLOCK), 0)
    col = lax.broadcasted_iota(jnp.int32, (BLOCK, BLOCK), 1)
    rel = (row - col).astype(_F32)
    causal = row >= col
    pos = lax.broadcasted_iota(jnp.int32, (BLOCK, 1), 0).astype(_F32)

    for h in range(RET_HEADS):
        lg = RET_LOG_GAMMA[h]
        q = _rope(_mm(u, w_ref[:, OFF_RQ + h * RET_QK:OFF_RQ + (h + 1) * RET_QK]), cos, sin)
        k = _rope(_mm(u, w_ref[:, OFF_RK + h * RET_QK:OFF_RK + (h + 1) * RET_QK]), cos, sin)
        k = k * (RET_QK ** -0.5)
        v = _mm(u, w_ref[:, OFF_RV + h * RET_V:OFF_RV + (h + 1) * RET_V]).astype(_BF16)
        g = _mm(u, w_ref[:, OFF_RG + h * RET_V:OFF_RG + (h + 1) * RET_V])
        qb = q.astype(_BF16)
        decay = jnp.where(causal, jnp.exp(jnp.maximum(rel, 0.0) * lg), 0.0)
        scores = (_mm_nt(qb, k.astype(_BF16)) * decay).astype(_BF16)
        state = sret_ref[h]
        xi = jnp.exp((pos + 1.0) * lg)
        zeta = jnp.exp((BLOCK - 1.0 - pos) * lg)
        o = _mm(scores, v) + _mm(qb, state.astype(_BF16)) * xi
        sret_ref[h] = state * math.exp(BLOCK * lg) + _mm_tn((k * zeta).astype(_BF16), v)
        mu = jnp.mean(o, axis=-1, keepdims=True)
        oc = o - mu
        var = jnp.mean(oc * oc, axis=-1, keepdims=True)
        on = oc * lax.rsqrt(var + EPS) * rgain_ref[:, h * RET_V:(h + 1) * RET_V]
        oret_ref[:, h * RET_V:(h + 1) * RET_V] = (on * (g * _sigmoid(g))).astype(_BF16)

    log_a = _gla_log_decay(u, w_ref, wgu_ref, bg_ref)
    n_sub = BLOCK // GLA_SUB
    same_sub = (row // GLA_SUB) == (col // GLA_SUB)
    tri = jnp.where(causal & same_sub, 1.0, 0.0).astype(_BF16)
    b_all = _cumsum_rows(log_a, tri)
    causal_sub = causal[:GLA_SUB, :GLA_SUB]
    mid = GLA_SUB // 2 - 1
    for h in range(GLA_HEADS):
        gq = _mm(u, w_ref[:, OFF_GQ + h * GLA_K:OFF_GQ + (h + 1) * GLA_K]) * (GLA_K ** -0.5)
        gk = _mm(u, w_ref[:, OFF_GK + h * GLA_K:OFF_GK + (h + 1) * GLA_K])
        gv = _mm(u, w_ref[:, OFF_GV + h * GLA_V:OFF_GV + (h + 1) * GLA_V]).astype(_BF16)
        gg = _mm(u, w_ref[:, OFF_GG + h * GLA_V:OFF_GG + (h + 1) * GLA_V])
        state_t = sgla_ref[h]
        outs = []
        for n in range(n_sub):
            r0 = n * GLA_SUB
            b = b_all[r0:r0 + GLA_SUB, h * GLA_K:(h + 1) * GLA_K]
            qs = gq[r0:r0 + GLA_SUB]
            ks = gk[r0:r0 + GLA_SUB]
            vs = gv[r0:r0 + GLA_SUB]
            b_mid = b[mid:mid + 1]
            b_last = b[GLA_SUB - 1:GLA_SUB]
            q_in = (qs * jnp.exp(b - b_mid)).astype(_BF16)
            k_in = (ks * jnp.exp(b_mid - b)).astype(_BF16)
            s = jnp.where(causal_sub, _mm_nt(q_in, k_in), 0.0).astype(_BF16)
            q_dec = (qs * jnp.exp(b)).astype(_BF16)
            outs.append(_mm(s, vs) + _mm_nt(q_dec, state_t.astype(_BF16)))
            k_end = (ks * jnp.exp(b_last - b)).astype(_BF16)
            state_t = state_t * jnp.exp(b_last) + _mm_tn(vs, k_end)
        sgla_ref[h] = state_t
        o = jnp.concatenate(outs, axis=0)
        on = _rms_norm(o, ggain_ref[:, h * GLA_V:(h + 1) * GLA_V])
        ogla_ref[:, h * GLA_V:(h + 1) * GLA_V] = (on * (gg * _sigmoid(gg))).astype(_BF16)

    m_ret = _sigmoid(_mm(u, w_ref[:, OFF_MR:OFF_MR + D_MODEL]))
    m_gla = _sigmoid(_mm(u, w_ref[:, OFF_MG:OFF_MG + D_MODEL]))
    merged = m_ret * _mm(oret_ref[...], wbr_ref[...]) + m_gla * _mm(ogla_ref[...], wbg_ref[...])
    y = x + _mm(merged.astype(_BF16), wout_ref[...])
    o_ref[...] = _rms_norm(y, fng_ref[...])


def _meta_kernel(x_ref, cos_ref, sin_ref, ng_ref, w_ref, wgu_ref, bg_ref, sret_ref, sgla_ref):
    rows = META_ROWS
    u = _rms_norm(x_ref[...], ng_ref[...]).astype(_BF16)
    cos = cos_ref[...]
    sin = sin_ref[...]
    row = lax.broadcasted_iota(jnp.int32, (rows, rows), 0)
    col = lax.broadcasted_iota(jnp.int32, (rows, rows), 1)
    pos = lax.broadcasted_iota(jnp.int32, (rows, 1), 0).astype(_F32)
    for h in range(RET_HEADS):
        lg = RET_LOG_GAMMA[h]
        k = _rope(_mm(u, w_ref[:, OFF_RK + h * RET_QK:OFF_RK + (h + 1) * RET_QK]), cos, sin)
        k = k * (RET_QK ** -0.5)
        v = _mm(u, w_ref[:, OFF_RV + h * RET_V:OFF_RV + (h + 1) * RET_V]).astype(_BF16)
        zeta = jnp.exp((rows - 1.0 - pos) * lg)
        sret_ref[h] = _mm_tn((k * zeta).astype(_BF16), v)
    log_a = _gla_log_decay(u, w_ref, wgu_ref, bg_ref)
    tri = jnp.where(row >= col, 1.0, 0.0).astype(_BF16)
    b_all = _cumsum_rows(log_a, tri)
    for h in range(GLA_HEADS):
        gk = _mm(u, w_ref[:, OFF_GK + h * GLA_K:OFF_GK + (h + 1) * GLA_K])
        gv = _mm(u, w_ref[:, OFF_GV + h * GLA_V:OFF_GV + (h + 1) * GLA_V]).astype(_BF16)
        b = b_all[:, h * GLA_K:(h + 1) * GLA_K]
        k_end = (gk * jnp.exp(b[rows - 1:rows] - b)).astype(_BF16)
        sgla_ref[h] = _mm_tn(gv, k_end)


def _rope_tables(first_pos, count):
    half = RET_QK // 2
    inv = ROPE_BASE ** (-np.arange(half, dtype=np.float64) / half)
    ang = (first_pos + np.arange(count, dtype=np.float64))[:, None] * inv[None, :]
    return jnp.asarray(np.cos(ang), _F32), jnp.asarray(np.sin(ang), _F32)


def _resident(shape):
    return pl.BlockSpec(shape, lambda *_: (0,) * len(shape), pipeline_mode=pl.Buffered(1))


def kernel(x, meta_tokens, norm_gain, w_in, w_gate_up, b_gate, ret_norm_gain, gla_norm_gain,
           w_branch_ret, w_branch_gla, w_out, final_norm_gain):
    bsz, seq, d = x.shape
    assert d == D_MODEL and seq % BLOCK == 0 and w_in.shape[0] == 1
    n_blocks = seq // BLOCK

    sizes = (RET_HEADS * RET_QK, RET_HEADS * RET_QK, RET_WIDTH, RET_WIDTH, GLA_HEADS * GLA_K,
             GLA_HEADS * GLA_K, GLA_WIDTH, GLA_WIDTH, GLA_RANK, D_MODEL, D_MODEL)
    points = [int(p) for p in np.cumsum(sizes)[:-1]]
    (rq, rk, rv, rg, gq, gk, gv, gg, glr, m_ret, m_gla) = jnp.split(w_in[0], points, axis=-1)
    glr = jnp.pad(glr, ((0, 0), (0, RANK_PAD - GLA_RANK)))
    w_all = jnp.concatenate([rq, rk, rv, rg, gq, gk, gv, gg, m_ret, m_gla, glr], axis=-1).astype(_BF16)
    assert w_all.shape == (D_MODEL, W_COLS)
    wgu = jnp.pad(w_gate_up[0], ((0, RANK_PAD - GLA_RANK), (0, 0))).astype(_BF16)
    bg = b_gate[0].reshape(1, -1)
    ng = norm_gain[0].reshape(1, -1)
    rgain = ret_norm_gain[0].reshape(1, -1)
    ggain = gla_norm_gain[0].reshape(1, -1)
    fng = final_norm_gain.reshape(1, -1)
    wbr = w_branch_ret[0].astype(_BF16)
    wbg = w_branch_gla[0].astype(_BF16)
    wout = w_out[0].astype(_BF16)

    cos_m, sin_m = _rope_tables(N_META - META_ROWS, META_ROWS)
    cos_x, sin_x = _rope_tables(N_META, seq)
    meta_pad = jnp.pad(meta_tokens.astype(_F32), ((META_ROWS - N_META, 0), (0, 0)))

    sret0, sgla0 = pl.pallas_call(
        _meta_kernel,
        out_shape=(jax.ShapeDtypeStruct((RET_HEADS, RET_QK, RET_V), _F32),
                   jax.ShapeDtypeStruct((GLA_HEADS, GLA_V, GLA_K), _F32)),
        compiler_params=pltpu.CompilerParams(vmem_limit_bytes=VMEM_LIMIT_BYTES),
        name="meta_state",
    )(meta_pad, cos_m, sin_m, ng, w_all, wgu, bg)

    in_specs = [
        pl.BlockSpec((None, BLOCK, D_MODEL), lambda b, j: (b, j, 0)),
        pl.BlockSpec((BLOCK, LANES), lambda b, j: (j, 0)),
        pl.BlockSpec((BLOCK, LANES), lambda b, j: (j, 0)),
        _resident((1, D_MODEL)),
        _resident((D_MODEL, W_COLS)),
        _resident((RANK_PAD, GLA_HEADS * GLA_K)),
        _resident((1, GLA_HEADS * GLA_K)),
        _resident((1, RET_WIDTH)),
        _resident((1, GLA_WIDTH)),
        _resident((RET_WIDTH, D_MODEL)),
        _resident((GLA_WIDTH, D_MODEL)),
        _resident((D_MODEL, D_MODEL)),
        _resident((1, D_MODEL)),
        _resident((RET_HEADS, RET_QK, RET_V)),
        _resident((GLA_HEADS, GLA_V, GLA_K)),
    ]
    return pl.pallas_call(
        _main_kernel,
        grid=(bsz, n_blocks),
        in_specs=in_specs,
        out_specs=pl.BlockSpec((None, BLOCK, D_MODEL), lambda b, j: (b, j, 0)),
        out_shape=jax.ShapeDtypeStruct((bsz, seq, D_MODEL), x.dtype),
        scratch_shapes=[
            pltpu.VMEM((RET_HEADS, RET_QK, RET_V), _F32),
            pltpu.VMEM((GLA_HEADS, GLA_V, GLA_K), _F32),
            pltpu.VMEM((BLOCK, RET_WIDTH), _BF16),
            pltpu.VMEM((BLOCK, GLA_WIDTH), _BF16),
        ],
        compiler_params=pltpu.CompilerParams(
            dimension_semantics=("arbitrary", "arbitrary"),
            vmem_limit_bytes=VMEM_LIMIT_BYTES),
        name="hybrid_layer",
    )(x, cos_x, sin_x, ng, w_all, wgu, bg, rgain, ggain, wbr, wbg, wout, fng, sret0, sgla0)
```

```python
import functools
import math

import numpy as np
import jax
import jax.numpy as jnp
from jax import lax
from jax.experimental import pallas as pl
from jax.experimental.pallas import tpu as pltpu

D_MODEL = 1024
N_META = 16
RET_HEADS = 4
RET_QK = 256
RET_V = 512
RET_WIDTH = RET_HEADS * RET_V
GLA_HEADS = 4
GLA_K = 128
GLA_V = 256
GLA_WIDTH = GLA_HEADS * GLA_V
GLA_RANK = 16
GLA_TAU = 16.0
ROPE_BASE = 10000.0
EPS = 1e-6

LANES = 128
BLOCK = 256
GLA_SUB = 64
META_ROWS = 128
RANK_PAD = LANES
VMEM_LIMIT_BYTES = 60 * 1024 * 1024

OFF_RQ = 0
OFF_RK = OFF_RQ + RET_HEADS * RET_QK
OFF_RV = OFF_RK + RET_HEADS * RET_QK
OFF_RG = OFF_RV + RET_WIDTH
OFF_GQ = OFF_RG + RET_WIDTH
OFF_GK = OFF_GQ + GLA_HEADS * GLA_K
OFF_GV = OFF_GK + GLA_HEADS * GLA_K
OFF_GG = OFF_GV + GLA_WIDTH
OFF_MR = OFF_GG + GLA_WIDTH
OFF_MG = OFF_MR + D_MODEL
OFF_LR = OFF_MG + D_MODEL
W_COLS = OFF_LR + RANK_PAD

RET_LOG_GAMMA = tuple(math.log1p(-(2.0 ** (-5.0 - h))) for h in range(RET_HEADS))

_F32 = jnp.float32
_BF16 = jnp.bfloat16


def _mm(a, b):
    return jnp.dot(a, b, preferred_element_type=_F32)


def _mm_nt(a, b):
    return lax.dot_general(a, b, (((1,), (1,)), ((), ())), preferred_element_type=_F32)


def _mm_tn(a, b):
    return lax.dot_general(a, b, (((0,), (0,)), ((), ())), preferred_element_type=_F32)


def _sigmoid(x):
    return 1.0 / (1.0 + jnp.exp(-x))


def _log_sigmoid(x):
    return jnp.minimum(x, 0.0) - jnp.log1p(jnp.exp(-jnp.abs(x)))


def _rms_norm(x, gain):
    return x * lax.rsqrt(jnp.mean(x * x, axis=-1, keepdims=True) + EPS) * gain


def _rope(t, cos, sin):
    t1 = t[:, :LANES]
    t2 = t[:, LANES:]
    return jnp.concatenate([t1 * cos - t2 * sin, t2 * cos + t1 * sin], axis=-1)


def _cumsum_rows(x, tri):
    hi = x.astype(_BF16)
    r1 = x - hi.astype(_F32)
    mid = r1.astype(_BF16)
    lo = (r1 - mid.astype(_F32)).astype(_BF16)
    return _mm(tri, hi) + _mm(tri, mid) + _mm(tri, lo)


def _gla_log_decay(u, w_ref, wgu_ref, bg_ref):
    glr = _mm(u, w_ref[:, OFF_LR:OFF_LR + RANK_PAD]).astype(_BF16)
    z = _mm(glr, wgu_ref[...]) + bg_ref[...]
    return _log_sigmoid(z) * (1.0 / GLA_TAU)


def _main_kernel(x_ref, cos_ref, sin_ref, ng_ref, w_ref, wgu_ref, bg_ref, rgain_ref, ggain_ref,
                 wbr_ref, wbg_ref, wout_ref, fng_ref, sret0_ref, sgla0_ref,
                 o_ref, sret_ref, sgla_ref, oret_ref, ogla_ref):
    j = pl.program_id(1)

    @pl.when(j == 0)
    def _():
        sret_ref[...] = sret0_ref[...]
        sgla_ref[...] = sgla0_ref[...]

    x = x_ref[...]
    u = _rms_norm(x, ng_ref[...]).astype(_BF16)
    cos = cos_ref[...]
    sin = sin_ref[...]
    pos1 = lax.broadcasted_iota(jnp.int32, (BLOCK, 1), 0).astype(_F32) + 1.0
    row = lax.broadcasted_iota(jnp.int32, (BLOCK, BLOCK), 0)
    col = lax.broadcasted_iota(jnp.int32, (BLOCK, BLOCK), 1)
    causal = row >= col

    rq = _mm(u, w_ref[:, OFF_RQ:OFF_RK])
    rk = _mm(u, w_ref[:, OFF_RK:OFF_RV])
    rv = _mm(u, w_ref[:, OFF_RV:OFF_RG]).astype(_BF16)
    qx, kz = [], []
    for h in range(RET_HEADS):
        lg = RET_LOG_GAMMA[h]
        xi = jnp.exp(pos1 * lg)
        zk = jnp.exp(pos1 * (-lg)) * (RET_QK ** -0.5)
        qx.append(_rope(rq[:, h * RET_QK:(h + 1) * RET_QK], cos * xi, sin * xi).astype(_BF16))
        kz.append(_rope(rk[:, h * RET_QK:(h + 1) * RET_QK], cos * zk, sin * zk).astype(_BF16))
    scores = [jnp.where(causal, _mm_nt(qx[h], kz[h]), 0.0).astype(_BF16) for h in range(RET_HEADS)]
    o_ret = []
    for h in range(RET_HEADS):
        v = rv[:, h * RET_V:(h + 1) * RET_V]
        lhs = jnp.concatenate([qx[h], scores[h]], axis=1)
        rhs = jnp.concatenate([sret_ref[h].astype(_BF16), v], axis=0)
        o_ret.append(_mm(lhs, rhs))
    for h in range(RET_HEADS):
        v = rv[:, h * RET_V:(h + 1) * RET_V]
        sret_ref[h] = (sret_ref[h] + _mm_tn(kz[h], v)) * math.exp(BLOCK * RET_LOG_GAMMA[h])

    glr = _mm(u, w_ref[:, OFF_LR:OFF_LR + RANK_PAD]).astype(_BF16)
    gq = _mm(u, w_ref[:, OFF_GQ:OFF_GK])
    gk = _mm(u, w_ref[:, OFF_GK:OFF_GV])
    gv = _mm(u, w_ref[:, OFF_GV:OFF_GG]).astype(_BF16)
    z = _mm(glr, wgu_ref[...]) + bg_ref[...]
    log_a = _log_sigmoid(z) * (1.0 / GLA_TAU)

    rg = _mm(u, w_ref[:, OFF_RG:OFF_GQ])
    for h in range(RET_HEADS):
        o = o_ret[h]
        g = rg[:, h * RET_V:(h + 1) * RET_V]
        mu = jnp.mean(o, axis=-1, keepdims=True)
        oc = o - mu
        var = jnp.mean(oc * oc, axis=-1, keepdims=True)
        on = oc * lax.rsqrt(var + EPS) * rgain_ref[:, h * RET_V:(h + 1) * RET_V]
        oret_ref[:, h * RET_V:(h + 1) * RET_V] = (on * (g * _sigmoid(g))).astype(_BF16)

    n_sub = BLOCK // GLA_SUB
    same_sub = (row // GLA_SUB) == (col // GLA_SUB)
    tri = jnp.where(causal & same_sub, 1.0, 0.0).astype(_BF16)
    b_all = _cumsum_rows(log_a, tri)
    gg = _mm(u, w_ref[:, OFF_GG:OFF_MR])
    m_ret = _sigmoid(_mm(u, w_ref[:, OFF_MR:OFF_MG]))
    causal_sub = causal[:GLA_SUB, :GLA_SUB]
    mid = GLA_SUB // 2 - 1
    ln_scale = math.log(GLA_K ** -0.5)
    q_dec = [[None] * n_sub for _ in range(GLA_HEADS)]
    s_in = [[None] * n_sub for _ in range(GLA_HEADS)]
    upd = [[None] * n_sub for _ in range(GLA_HEADS)]
    dcol = [[None] * n_sub for _ in range(GLA_HEADS)]
    for n in range(n_sub):
        r0 = n * GLA_SUB
        for h in range(GLA_HEADS):
            c0 = h * GLA_K
            b = b_all[r0:r0 + GLA_SUB, c0:c0 + GLA_K]
            b_mid = b[mid:mid + 1]
            b_last = b[GLA_SUB - 1:GLA_SUB]
            q_in = gq[r0:r0 + GLA_SUB, c0:c0 + GLA_K] * jnp.exp(b - (b_mid - ln_scale))
            k_in = gk[r0:r0 + GLA_SUB, c0:c0 + GLA_K] * jnp.exp(b_mid - b)
            q_dec[h][n] = (q_in * jnp.exp(b_mid)).astype(_BF16)
            k_end = (k_in * jnp.exp(b_last - b_mid)).astype(_BF16)
            s_in[h][n] = jnp.where(causal_sub, _mm_nt(q_in.astype(_BF16), k_in.astype(_BF16)), 0.0).astype(_BF16)
            upd[h][n] = _mm_tn(k_end, gv[r0:r0 + GLA_SUB, h * GLA_V:(h + 1) * GLA_V])
            dcol[h][n] = jnp.transpose(jnp.exp(b_last))
    m_gla = _sigmoid(_mm(u, w_ref[:, OFF_MG:OFF_LR]))
    o_gla = [[None] * n_sub for _ in range(GLA_HEADS)]
    for h in range(GLA_HEADS):
        state = sgla_ref[h]
        for n in range(n_sub):
            r0 = n * GLA_SUB
            lhs = jnp.concatenate([q_dec[h][n], s_in[h][n]], axis=1)
            rhs = jnp.concatenate([state.astype(_BF16), gv[r0:r0 + GLA_SUB, h * GLA_V:(h + 1) * GLA_V]], axis=0)
            o_gla[h][n] = _mm(lhs, rhs)
            state = state * dcol[h][n] + upd[h][n]
        sgla_ref[h] = state
    for h in range(GLA_HEADS):
        o = jnp.concatenate(o_gla[h], axis=0)
        g = gg[:, h * GLA_V:(h + 1) * GLA_V]
        on = _rms_norm(o, ggain_ref[:, h * GLA_V:(h + 1) * GLA_V])
        ogla_ref[:, h * GLA_V:(h + 1) * GLA_V] = (on * (g * _sigmoid(g))).astype(_BF16)

    merged = m_ret * _mm(oret_ref[...], wbr_ref[...]) + m_gla * _mm(ogla_ref[...], wbg_ref[...])
    y = x + _mm(merged.astype(_BF16), wout_ref[...])
    o_ref[...] = _rms_norm(y, fng_ref[...])


def _meta_kernel(x_ref, cos_ref, sin_ref, ng_ref, w_ref, wgu_ref, bg_ref, sret_ref, sgla_ref):
    rows = META_ROWS
    u = _rms_norm(x_ref[...], ng_ref[...]).astype(_BF16)
    cos = cos_ref[...]
    sin = sin_ref[...]
    row = lax.broadcasted_iota(jnp.int32, (rows, rows), 0)
    col = lax.broadcasted_iota(jnp.int32, (rows, rows), 1)
    pos = lax.broadcasted_iota(jnp.int32, (rows, 1), 0).astype(_F32)
    for h in range(RET_HEADS):
        lg = RET_LOG_GAMMA[h]
        k = _rope(_mm(u, w_ref[:, OFF_RK + h * RET_QK:OFF_RK + (h + 1) * RET_QK]), cos, sin)
        k = k * (RET_QK ** -0.5)
        v = _mm(u, w_ref[:, OFF_RV + h * RET_V:OFF_RV + (h + 1) * RET_V]).astype(_BF16)
        zeta = jnp.exp((rows - 1.0 - pos) * lg)
        sret_ref[h] = _mm_tn((k * zeta).astype(_BF16), v)
    log_a = _gla_log_decay(u, w_ref, wgu_ref, bg_ref)
    tri = jnp.where(row >= col, 1.0, 0.0).astype(_BF16)
    b_all = _cumsum_rows(log_a, tri)
    for h in range(GLA_HEADS):
        gk = _mm(u, w_ref[:, OFF_GK + h * GLA_K:OFF_GK + (h + 1) * GLA_K])
        gv = _mm(u, w_ref[:, OFF_GV + h * GLA_V:OFF_GV + (h + 1) * GLA_V]).astype(_BF16)
        b = b_all[:, h * GLA_K:(h + 1) * GLA_K]
        k_end = (gk * jnp.exp(b[rows - 1:rows] - b)).astype(_BF16)
        sgla_ref[h] = _mm_tn(k_end, gv)


def _rope_tables(first_pos, count):
    half = RET_QK // 2
    inv = ROPE_BASE ** (-np.arange(half, dtype=np.float64) / half)
    ang = (first_pos + np.arange(count, dtype=np.float64))[:, None] * inv[None, :]
    return jnp.asarray(np.cos(ang), _F32), jnp.asarray(np.sin(ang), _F32)


def _resident(shape):
    return pl.BlockSpec(shape, lambda *_: (0,) * len(shape), pipeline_mode=pl.Buffered(1))


def kernel(x, meta_tokens, norm_gain, w_in, w_gate_up, b_gate, ret_norm_gain, gla_norm_gain,
           w_branch_ret, w_branch_gla, w_out, final_norm_gain):
    bsz, seq, d = x.shape
    assert d == D_MODEL and seq % BLOCK == 0 and w_in.shape[0] == 1
    n_blocks = seq // BLOCK

    sizes = (RET_HEADS * RET_QK, RET_HEADS * RET_QK, RET_WIDTH, RET_WIDTH, GLA_HEADS * GLA_K,
             GLA_HEADS * GLA_K, GLA_WIDTH, GLA_WIDTH, GLA_RANK, D_MODEL, D_MODEL)
    points = [int(p) for p in np.cumsum(sizes)[:-1]]
    (rq, rk, rv, rg, gq, gk, gv, gg, glr, m_ret, m_gla) = jnp.split(w_in[0], points, axis=-1)
    glr = jnp.pad(glr, ((0, 0), (0, RANK_PAD - GLA_RANK)))
    w_all = jnp.concatenate([rq, rk, rv, rg, gq, gk, gv, gg, m_ret, m_gla, glr], axis=-1).astype(_BF16)
    assert w_all.shape == (D_MODEL, W_COLS)
    wgu = jnp.pad(w_gate_up[0], ((0, RANK_PAD - GLA_RANK), (0, 0))).astype(_BF16)
    bg = b_gate[0].reshape(1, -1)
    ng = norm_gain[0].reshape(1, -1)
    rgain = ret_norm_gain[0].reshape(1, -1)
    ggain = gla_norm_gain[0].reshape(1, -1)
    fng = final_norm_gain.reshape(1, -1)
    wbr = w_branch_ret[0].astype(_BF16)
    wbg = w_branch_gla[0].astype(_BF16)
    wout = w_out[0].astype(_BF16)

    cos_m, sin_m = _rope_tables(N_META - META_ROWS, META_ROWS)
    cos_x, sin_x = _rope_tables(N_META, seq)
    meta_pad = jnp.pad(meta_tokens.astype(_F32), ((META_ROWS - N_META, 0), (0, 0)))

    sret0, sgla0 = pl.pallas_call(
        _meta_kernel,
        out_shape=(jax.ShapeDtypeStruct((RET_HEADS, RET_QK, RET_V), _F32),
                   jax.ShapeDtypeStruct((GLA_HEADS, GLA_K, GLA_V), _F32)),
        compiler_params=pltpu.CompilerParams(vmem_limit_bytes=VMEM_LIMIT_BYTES),
        name="meta_state",
    )(meta_pad, cos_m, sin_m, ng, w_all, wgu, bg)

    in_specs = [
        pl.BlockSpec((None, BLOCK, D_MODEL), lambda b, j: (b, j, 0)),
        pl.BlockSpec((BLOCK, LANES), lambda b, j: (j, 0)),
        pl.BlockSpec((BLOCK, LANES), lambda b, j: (j, 0)),
        _resident((1, D_MODEL)),
        _resident((D_MODEL, W_COLS)),
        _resident((RANK_PAD, GLA_HEADS * GLA_K)),
        _resident((1, GLA_HEADS * GLA_K)),
        _resident((1, RET_WIDTH)),
        _resident((1, GLA_WIDTH)),
        _resident((RET_WIDTH, D_MODEL)),
        _resident((GLA_WIDTH, D_MODEL)),
        _resident((D_MODEL, D_MODEL)),
        _resident((1, D_MODEL)),
        _resident((RET_HEADS, RET_QK, RET_V)),
        _resident((GLA_HEADS, GLA_K, GLA_V)),
    ]
    return pl.pallas_call(
        _main_kernel,
        grid=(bsz, n_blocks),
        in_specs=in_specs,
        out_specs=pl.BlockSpec((None, BLOCK, D_MODEL), lambda b, j: (b, j, 0)),
        out_shape=jax.ShapeDtypeStruct((bsz, seq, D_MODEL), x.dtype),
        scratch_shapes=[
            pltpu.VMEM((RET_HEADS, RET_QK, RET_V), _F32),
            pltpu.VMEM((GLA_HEADS, GLA_K, GLA_V), _F32),
            pltpu.VMEM((BLOCK, RET_WIDTH), _BF16),
            pltpu.VMEM((BLOCK, GLA_WIDTH), _BF16),
        ],
        compiler_params=pltpu.CompilerParams(
            dimension_semantics=("arbitrary", "arbitrary"),
            vmem_limit_bytes=VMEM_LIMIT_BYTES),
        name="hybrid_layer",
    )(x, cos_x, sin_x, ng, w_all, wgu, bg, rgain, ggain, wbr, wbg, wout, fng, sret0, sgla0)
```

```python
import functools
import math

import numpy as np
import jax
import jax.numpy as jnp
from jax import lax
from jax.experimental import pallas as pl
from jax.experimental.pallas import tpu as pltpu

D_MODEL = 1024
N_META = 16
RET_HEADS = 4
RET_QK = 256
RET_V = 512
RET_WIDTH = RET_HEADS * RET_V
GLA_HEADS = 4
GLA_K = 128
GLA_V = 256
GLA_WIDTH = GLA_HEADS * GLA_V
GLA_RANK = 16
GLA_TAU = 16.0
ROPE_BASE = 10000.0
EPS = 1e-6

LANES = 128
BLOCK = 256
GLA_SUB = 64
META_ROWS = 128
RANK_PAD = LANES
VMEM_LIMIT_BYTES = 60 * 1024 * 1024

OFF_RQ = 0
OFF_RK = OFF_RQ + RET_HEADS * RET_QK
OFF_RV = OFF_RK + RET_HEADS * RET_QK
OFF_RG = OFF_RV + RET_WIDTH
OFF_GQ = OFF_RG + RET_WIDTH
OFF_GK = OFF_GQ + GLA_HEADS * GLA_K
OFF_GV = OFF_GK + GLA_HEADS * GLA_K
OFF_GG = OFF_GV + GLA_WIDTH
OFF_MR = OFF_GG + GLA_WIDTH
OFF_MG = OFF_MR + D_MODEL
OFF_LR = OFF_MG + D_MODEL
W_COLS = OFF_LR + RANK_PAD

RET_LOG_GAMMA = tuple(math.log1p(-(2.0 ** (-5.0 - h))) for h in range(RET_HEADS))

_F32 = jnp.float32
_BF16 = jnp.bfloat16


def _mm(a, b):
    return jnp.dot(a, b, preferred_element_type=_F32)


def _mm_nt(a, b):
    return lax.dot_general(a, b, (((1,), (1,)), ((), ())), preferred_element_type=_F32)


def _mm_tn(a, b):
    return lax.dot_general(a, b, (((0,), (0,)), ((), ())), preferred_element_type=_F32)


def _sigmoid(x):
    return 1.0 / (1.0 + jnp.exp(-x))


def _log_sigmoid(x):
    return jnp.minimum(x, 0.0) - jnp.log(1.0 + jnp.exp(-jnp.abs(x)))


def _rms_norm(x, gain):
    return x * lax.rsqrt(jnp.mean(x * x, axis=-1, keepdims=True) + EPS) * gain


def _rope(t, cos, sin):
    t1 = t[:, :LANES]
    t2 = t[:, LANES:]
    return jnp.concatenate([t1 * cos - t2 * sin, t2 * cos + t1 * sin], axis=-1)


def _cumsum_rows(x, tri):
    hi = x.astype(_BF16)
    r1 = x - hi.astype(_F32)
    mid = r1.astype(_BF16)
    lo = (r1 - mid.astype(_F32)).astype(_BF16)
    return _mm(tri, hi) + _mm(tri, mid) + _mm(tri, lo)


def _gla_log_decay(u, w_ref, wgu_ref, bg_ref):
    glr = _mm(u, w_ref[:, OFF_LR:OFF_LR + RANK_PAD]).astype(_BF16)
    z = _mm(glr, wgu_ref[...]) + bg_ref[...]
    return _log_sigmoid(z) * (1.0 / GLA_TAU)


def _main_kernel(n_seq_blocks, xcur_ref, xnxt_ref, cos_ref, sin_ref, ng_ref, w_ref, wgu_ref, bg_ref,
                 rgain_ref, ggain_ref, wbr_ref, wbg_ref, wout_ref, fng_ref, sret0_ref, sgla0_ref,
                 o_ref, sret_ref, sgla_ref, oret_ref, ogla_ref, u_s, qx_s, kz_s, rv_s):
    s = pl.program_id(0)
    cur = jnp.maximum(s - 1, 0)

    @pl.when(s == 0)
    def _():
        u_s[...] = jnp.zeros_like(u_s)
        qx_s[...] = jnp.zeros_like(qx_s)
        kz_s[...] = jnp.zeros_like(kz_s)
        rv_s[...] = jnp.zeros_like(rv_s)

    @pl.when(lax.rem(cur, n_seq_blocks) == 0)
    def _():
        sret_ref[...] = sret0_ref[...]
        sgla_ref[...] = sgla0_ref[...]

    x = xcur_ref[...]
    u = u_s[...]
    row = lax.broadcasted_iota(jnp.int32, (BLOCK, BLOCK), 0)
    col = lax.broadcasted_iota(jnp.int32, (BLOCK, BLOCK), 1)
    causal = row >= col

    qx = [qx_s[:, h * RET_QK:(h + 1) * RET_QK] for h in range(RET_HEADS)]
    kz = [kz_s[:, h * RET_QK:(h + 1) * RET_QK] for h in range(RET_HEADS)]
    rv = rv_s[...]
    scores = [jnp.where(causal, _mm_nt(qx[h], kz[h]), 0.0).astype(_BF16) for h in range(RET_HEADS)]
    glr = _mm(u, w_ref[:, OFF_LR:OFF_LR + RANK_PAD]).astype(_BF16)
    gq = _mm(u, w_ref[:, OFF_GQ:OFF_GK])
    gk = _mm(u, w_ref[:, OFF_GK:OFF_GV])
    o_ret = []
    for h in range(RET_HEADS):
        v = rv[:, h * RET_V:(h + 1) * RET_V]
        lhs = jnp.concatenate([qx[h], scores[h]], axis=1)
        rhs = jnp.concatenate([sret_ref[h].astype(_BF16), v], axis=0)
        o_ret.append(_mm(lhs, rhs))
    for h in range(RET_HEADS):
        v = rv[:, h * RET_V:(h + 1) * RET_V]
        sret_ref[h] = (sret_ref[h] + _mm_tn(kz[h], v)) * math.exp(BLOCK * RET_LOG_GAMMA[h])

    gv =_mm(u, w_ref[:, OFF_GV:OFF_GG]).astype(_BF16)
    z = _mm(glr, wgu_ref[...]) + bg_ref[...]
    log_a = _log_sigmoid(z) * (1.0 / GLA_TAU)

    rg = _mm(u, w_ref[:, OFF_RG:OFF_GQ])
    for h in range(RET_HEADS):
        o = o_ret[h]
        g = rg[:, h * RET_V:(h + 1) * RET_V]
        mu = jnp.mean(o, axis=-1, keepdims=True)
        oc = o - mu
        var = jnp.mean(oc * oc, axis=-1, keepdims=True)
        on = oc * lax.rsqrt(var + EPS) * rgain_ref[:, h * RET_V:(h + 1) * RET_V]
        oret_ref[:, h * RET_V:(h + 1) * RET_V] = (on * (g * _sigmoid(g))).astype(_BF16)

    n_sub = BLOCK // GLA_SUB
    same_sub = (row // GLA_SUB) == (col // GLA_SUB)
    tri = jnp.where(causal & same_sub, 1.0, 0.0).astype(_BF16)
    b_all = _cumsum_rows(log_a, tri)
    gg = _mm(u, w_ref[:, OFF_GG:OFF_MR])
    m_ret = _sigmoid(_mm(u, w_ref[:, OFF_MR:OFF_MG]))
    br = _mm(oret_ref[...], wbr_ref[...])
    causal_sub = causal[:GLA_SUB, :GLA_SUB]
    mid = GLA_SUB // 2 - 1
    ln_scale = math.log(GLA_K ** -0.5)
    q_dec = [[None] * n_sub for _ in range(GLA_HEADS)]
    s_in = [[None] * n_sub for _ in range(GLA_HEADS)]
    upd = [[None] * n_sub for _ in range(GLA_HEADS)]
    dcol = [[None] * n_sub for _ in range(GLA_HEADS)]
    for n in range(n_sub):
        r0 = n * GLA_SUB
        for h in range(GLA_HEADS):
            c0 = h * GLA_K
            b = b_all[r0:r0 + GLA_SUB, c0:c0 + GLA_K]
            b_mid = b[mid:mid + 1]
            b_last = b[GLA_SUB - 1:GLA_SUB]
            q_in = gq[r0:r0 + GLA_SUB, c0:c0 + GLA_K] * jnp.exp(b - (b_mid - ln_scale))
            k_in = gk[r0:r0 + GLA_SUB, c0:c0 + GLA_K] * jnp.exp(b_mid - b)
            q_dec[h][n] = (q_in * jnp.exp(b_mid)).astype(_BF16)
            k_end = (k_in * jnp.exp(b_last - b_mid)).astype(_BF16)
            s_in[h][n] = jnp.where(causal_sub, _mm_nt(q_in.astype(_BF16), k_in.astype(_BF16)), 0.0).astype(_BF16)
            upd[h][n] = _mm_tn(k_end, gv[r0:r0 + GLA_SUB, h * GLA_V:(h + 1) * GLA_V])
            dcol[h][n] = jnp.transpose(jnp.exp(b_last))
    o_gla =[[None] * n_sub for _ in range(GLA_HEADS)]
    for h in range(GLA_HEADS):
        state = sgla_ref[h]
        for n in range(n_sub):
            r0 = n * GLA_SUB
            lhs = jnp.concatenate([q_dec[h][n], s_in[h][n]], axis=1)
            rhs = jnp.concatenate([state.astype(_BF16), gv[r0:r0 + GLA_SUB, h * GLA_V:(h + 1) * GLA_V]], axis=0)
            o_gla[h][n] = _mm(lhs, rhs)
            state = state * dcol[h][n] + upd[h][n]
        sgla_ref[h] = state
    m_gla = _sigmoid(_mm(u, w_ref[:, OFF_MG:OFF_LR]))
    for h in range(GLA_HEADS):
        o = jnp.concatenate(o_gla[h], axis=0)
        g = gg[:, h * GLA_V:(h + 1) * GLA_V]
        on = _rms_norm(o, ggain_ref[:, h * GLA_V:(h + 1) * GLA_V])
        ogla_ref[:, h * GLA_V:(h + 1) * GLA_V] = (on * (g * _sigmoid(g))).astype(_BF16)

    un = _rms_norm(xnxt_ref[...], ng_ref[...]).astype(_BF16)
    cos = cos_ref[...]
    sin = sin_ref[...]
    pos1 = lax.broadcasted_iota(jnp.int32, (BLOCK, 1), 0).astype(_F32) + 1.0

    def rope_heads(t, sign, scale):
        outs = []
        for h in range(RET_HEADS):
            f = jnp.exp(pos1 * (sign * RET_LOG_GAMMA[h])) * scale
            outs.append(_rope(t[:, h * RET_QK:(h + 1) * RET_QK], cos * f, sin * f).astype(_BF16))
        return jnp.concatenate(outs, axis=1)

    qx_n = rope_heads(_mm(un, w_ref[:, OFF_RQ:OFF_RK]), 1.0, 1.0)
    bgl = _mm(ogla_ref[...], wbg_ref[...])
    kz_n = rope_heads(_mm(un, w_ref[:, OFF_RK:OFF_RV]), -1.0, RET_QK ** -0.5)
    merged = m_ret * br + m_gla * bgl
    y = x + _mm(merged.astype(_BF16), wout_ref[...])
    rv_n = _mm(un, w_ref[:, OFF_RV:OFF_RG]).astype(_BF16)
    o_ref[...] = _rms_norm(y, fng_ref[...])
    u_s[...] = un
    qx_s[...] = qx_n
    kz_s[...] = kz_n
    rv_s[...] = rv_n


def _meta_kernel(x_ref, cos_ref, sin_ref, ng_ref, w_ref, wgu_ref, bg_ref, sret_ref, sgla_ref):
    rows = META_ROWS
    u = _rms_norm(x_ref[...], ng_ref[...]).astype(_BF16)
    cos = cos_ref[...]
    sin = sin_ref[...]
    row = lax.broadcasted_iota(jnp.int32, (rows, rows), 0)
    col = lax.broadcasted_iota(jnp.int32, (rows, rows), 1)
    pos = lax.broadcasted_iota(jnp.int32, (rows, 1), 0).astype(_F32)
    for h in range(RET_HEADS):
        lg = RET_LOG_GAMMA[h]
        k = _rope(_mm(u, w_ref[:, OFF_RK + h * RET_QK:OFF_RK + (h + 1) * RET_QK]), cos, sin)
        k = k * (RET_QK ** -0.5)
        v = _mm(u, w_ref[:, OFF_RV + h * RET_V:OFF_RV + (h + 1) * RET_V]).astype(_BF16)
        zeta = jnp.exp((rows - 1.0 - pos) * lg)
        sret_ref[h] = _mm_tn((k * zeta).astype(_BF16), v)
    log_a = _gla_log_decay(u, w_ref, wgu_ref, bg_ref)
    tri = jnp.where(row >= col, 1.0, 0.0).astype(_BF16)
    b_all = _cumsum_rows(log_a, tri)
    for h in range(GLA_HEADS):
        gk = _mm(u, w_ref[:, OFF_GK + h * GLA_K:OFF_GK + (h + 1) * GLA_K])
        gv = _mm(u, w_ref[:, OFF_GV + h * GLA_V:OFF_GV + (h + 1) * GLA_V]).astype(_BF16)
        b = b_all[:, h * GLA_K:(h + 1) * GLA_K]
        k_end = (gk * jnp.exp(b[rows - 1:rows] - b)).astype(_BF16)
        sgla_ref[h] = _mm_tn(k_end, gv)


def _rope_tables(first_pos, count):
    half = RET_QK // 2
    inv = ROPE_BASE ** (-np.arange(half, dtype=np.float64) / half)
    ang = (first_pos + np.arange(count, dtype=np.float64))[:, None] * inv[None, :]
    return jnp.asarray(np.cos(ang), _F32), jnp.asarray(np.sin(ang), _F32)


def _resident(shape):
    return pl.BlockSpec(shape, lambda *_: (0,) * len(shape), pipeline_mode=pl.Buffered(1))


def kernel(x, meta_tokens, norm_gain, w_in, w_gate_up, b_gate, ret_norm_gain, gla_norm_gain,
           w_branch_ret, w_branch_gla, w_out, final_norm_gain):
    bsz, seq, d = x.shape
    assert d == D_MODEL and seq % BLOCK == 0 and w_in.shape[0] == 1
    n_blocks = seq // BLOCK

    sizes = (RET_HEADS * RET_QK, RET_HEADS * RET_QK, RET_WIDTH, RET_WIDTH, GLA_HEADS * GLA_K,
             GLA_HEADS * GLA_K, GLA_WIDTH, GLA_WIDTH, GLA_RANK, D_MODEL, D_MODEL)
    points = [int(p) for p in np.cumsum(sizes)[:-1]]
    (rq, rk, rv, rg, gq, gk, gv, gg, glr, m_ret, m_gla) = jnp.split(w_in[0], points, axis=-1)
    glr = jnp.pad(glr, ((0, 0), (0, RANK_PAD - GLA_RANK)))
    w_all = jnp.concatenate([rq, rk, rv, rg, gq, gk, gv, gg, m_ret, m_gla, glr], axis=-1).astype(_BF16)
    assert w_all.shape == (D_MODEL, W_COLS)
    wgu = jnp.pad(w_gate_up[0], ((0, RANK_PAD - GLA_RANK), (0, 0))).astype(_BF16)
    bg = b_gate[0].reshape(1, -1)
    ng = norm_gain[0].reshape(1, -1)
    rgain = ret_norm_gain[0].reshape(1, -1)
    ggain = gla_norm_gain[0].reshape(1, -1)
    fng = final_norm_gain.reshape(1, -1)
    wbr = w_branch_ret[0].astype(_BF16)
    wbg = w_branch_gla[0].astype(_BF16)
    wout = w_out[0].astype(_BF16)

    cos_m, sin_m = _rope_tables(N_META - META_ROWS, META_ROWS)
    cos_x, sin_x = _rope_tables(N_META, seq)
    meta_pad = jnp.pad(meta_tokens.astype(_F32), ((META_ROWS - N_META, 0), (0, 0)))

    sret0, sgla0 = pl.pallas_call(
        _meta_kernel,
        out_shape=(jax.ShapeDtypeStruct((RET_HEADS, RET_QK, RET_V), _F32),
                   jax.ShapeDtypeStruct((GLA_HEADS, GLA_K, GLA_V), _F32)),
        compiler_params=pltpu.CompilerParams(vmem_limit_bytes=VMEM_LIMIT_BYTES),
        name="meta_state",
    )(meta_pad, cos_m, sin_m, ng, w_all, wgu, bg)

    n_total = bsz * n_blocks
    x2 = x.reshape(bsz * seq, D_MODEL)
    in_specs = [
        pl.BlockSpec((BLOCK, D_MODEL), lambda s: (jnp.maximum(s - 1, 0), 0)),
        pl.BlockSpec((BLOCK, D_MODEL), lambda s: (jnp.minimum(s, n_total - 1), 0)),
        pl.BlockSpec((BLOCK, LANES), lambda s: (lax.rem(jnp.minimum(s, n_total - 1), n_blocks), 0)),
        pl.BlockSpec((BLOCK, LANES), lambda s: (lax.rem(jnp.minimum(s, n_total - 1), n_blocks), 0)),
        _resident((1, D_MODEL)),
        _resident((D_MODEL, W_COLS)),
        _resident((RANK_PAD, GLA_HEADS * GLA_K)),
        _resident((1, GLA_HEADS * GLA_K)),
        _resident((1, RET_WIDTH)),
        _resident((1, GLA_WIDTH)),
        _resident((RET_WIDTH, D_MODEL)),
        _resident((GLA_WIDTH, D_MODEL)),
        _resident((D_MODEL, D_MODEL)),
        _resident((1, D_MODEL)),
        _resident((RET_HEADS, RET_QK, RET_V)),
        _resident((GLA_HEADS, GLA_K, GLA_V)),
    ]
    out = pl.pallas_call(
        functools.partial(_main_kernel, n_blocks),
        grid=(n_total + 1,),
        in_specs=in_specs,
        out_specs=pl.BlockSpec((BLOCK, D_MODEL), lambda s: (jnp.maximum(s - 1, 0), 0)),
        out_shape=jax.ShapeDtypeStruct((bsz * seq, D_MODEL), x.dtype),
        scratch_shapes=[
            pltpu.VMEM((RET_HEADS, RET_QK, RET_V), _F32),
            pltpu.VMEM((GLA_HEADS, GLA_K, GLA_V), _F32),
            pltpu.VMEM((BLOCK, RET_WIDTH), _BF16),
            pltpu.VMEM((BLOCK, GLA_WIDTH), _BF16),
            pltpu.VMEM((BLOCK, D_MODEL), _BF16),
            pltpu.VMEM((BLOCK, RET_HEADS * RET_QK), _BF16),
            pltpu.VMEM((BLOCK, RET_HEADS * RET_QK), _BF16),
            pltpu.VMEM((BLOCK, RET_WIDTH), _BF16),
        ],
        compiler_params=pltpu.CompilerParams(
            dimension_semantics=("arbitrary",),
            vmem_limit_bytes=VMEM_LIMIT_BYTES),
        name="hybrid_layer",
    )(x2, x2, cos_x, sin_x, ng, w_all, wgu, bg, rgain, ggain, wbr, wbg, wout, fng, sret0, sgla0)
    return out.reshape(bsz, seq, D_MODEL)
```

```python
import functools
import math

import numpy as np
import jax
import jax.numpy as jnp
from jax import lax
from jax.experimental import pallas as pl
from jax.experimental.pallas import tpu as pltpu

D_MODEL = 1024
N_META = 16
RET_HEADS = 4
RET_QK = 256
RET_V = 512
RET_WIDTH = RET_HEADS * RET_V
GLA_HEADS = 4
GLA_K = 128
GLA_V = 256
GLA_WIDTH = GLA_HEADS * GLA_V
GLA_RANK = 16
GLA_TAU = 16.0
ROPE_BASE = 10000.0
EPS = 1e-6

LANES = 128
BLOCK = 256
GLA_SUB = 64
META_ROWS = 128
RANK_PAD = LANES
VMEM_LIMIT_BYTES = 60 * 1024 * 1024

OFF_RQ = 0
OFF_RK = OFF_RQ + RET_HEADS * RET_QK
OFF_RV = OFF_RK + RET_HEADS * RET_QK
OFF_RG = OFF_RV + RET_WIDTH
OFF_GQ = OFF_RG + RET_WIDTH
OFF_GK = OFF_GQ + GLA_HEADS * GLA_K
OFF_GV = OFF_GK + GLA_HEADS * GLA_K
OFF_GG = OFF_GV + GLA_WIDTH
W_COLS = OFF_GG + GLA_WIDTH
SRC_LR = W_COLS
SRC_MR = SRC_LR + GLA_RANK

RET_LOG_GAMMA = tuple(math.log1p(-(2.0 ** (-5.0 - h))) for h in range(RET_HEADS))

_F32 = jnp.float32
_BF16 = jnp.bfloat16


def _mm(a, b):
    return jnp.dot(a, b, preferred_element_type=_F32)


def _mm_nt(a, b):
    return lax.dot_general(a, b, (((1,), (1,)), ((), ())), preferred_element_type=_F32)


def _mm_tn(a, b):
    return lax.dot_general(a, b, (((0,), (0,)), ((), ())), preferred_element_type=_F32)


def _sigmoid(x):
    return 1.0 / (1.0 + jnp.exp(-x))


def _log_sigmoid(x):
    return jnp.minimum(x, 0.0) - jnp.log(1.0 + jnp.exp(-jnp.abs(x)))


def _rms_norm(x, gain):
    return x * lax.rsqrt(jnp.mean(x * x, axis=-1, keepdims=True) + EPS) * gain


def _rope(t, cos, sin):
    t1 = t[:, :LANES]
    t2 = t[:, LANES:]
    return jnp.concatenate([t1 * cos - t2 * sin, t2 * cos + t1 * sin], axis=-1)


def _cumsum_rows(x, tri):
    hi = x.astype(_BF16)
    r1 = x - hi.astype(_F32)
    mid = r1.astype(_BF16)
    lo = (r1 - mid.astype(_F32)).astype(_BF16)
    return _mm(tri, hi) + _mm(tri, mid) + _mm(tri, lo)


def _gla_log_decay(u, wlr_ref, wgu_ref, bg_ref):
    glr = _mm(u, wlr_ref[...]).astype(_BF16)
    z = _mm(glr, wgu_ref[...]) + bg_ref[...]
    return _log_sigmoid(z) * (1.0 / GLA_TAU)


def _main_kernel(n_seq_blocks, xcur_ref, xnxt_ref, cos_ref, sin_ref, ng_ref, w_ref, wm_ref, wlr_ref, wgu_ref, bg_ref,
                 rgain_ref, ggain_ref, wbr_ref, wbg_ref, wout_ref, fng_ref, sret0_ref, sgla0_ref,
                 o_ref, sret_ref, sgla_ref, oret_ref, ogla_ref, u_s, qx_s, kz_s, rv_s):
    s = pl.program_id(0)
    cur = jnp.maximum(s - 1, 0)

    @pl.when(s == 0)
    def _():
        u_s[...] = jnp.zeros_like(u_s)
        qx_s[...] = jnp.zeros_like(qx_s)
        kz_s[...] = jnp.zeros_like(kz_s)
        rv_s[...] = jnp.zeros_like(rv_s)

    @pl.when(lax.rem(cur, n_seq_blocks) == 0)
    def _():
        sret_ref[...] = sret0_ref[...]
        sgla_ref[...] = sgla0_ref[...]

    x = xcur_ref[...]
    u = u_s[...]
    row = lax.broadcasted_iota(jnp.int32, (BLOCK, BLOCK), 0)
    col = lax.broadcasted_iota(jnp.int32, (BLOCK, BLOCK), 1)
    causal = row >= col

    qx = [qx_s[:, h * RET_QK:(h + 1) * RET_QK] for h in range(RET_HEADS)]
    kz = [kz_s[:, h * RET_QK:(h + 1) * RET_QK] for h in range(RET_HEADS)]
    rv = rv_s[...]
    scores = [jnp.where(causal, _mm_nt(qx[h], kz[h]), 0.0).astype(_BF16) for h in range(RET_HEADS)]
    glr = _mm(u, wlr_ref[...]).astype(_BF16)
    gq = _mm(u, w_ref[:, OFF_GQ:OFF_GK])
    gk = _mm(u, w_ref[:, OFF_GK:OFF_GV])
    o_ret = []
    for h in range(RET_HEADS):
        v = rv[:, h * RET_V:(h + 1) * RET_V]
        lhs = jnp.concatenate([qx[h], scores[h]], axis=1)
        rhs = jnp.concatenate([sret_ref[h].astype(_BF16), v], axis=0)
        o_ret.append(_mm(lhs, rhs))
    for h in range(RET_HEADS):
        v = rv[:, h * RET_V:(h + 1) * RET_V]
        sret_ref[h] = (sret_ref[h] + _mm_tn(kz[h], v)) * math.exp(BLOCK * RET_LOG_GAMMA[h])

    gv = _mm(u, w_ref[:, OFF_GV:OFF_GG]).astype(_BF16)
    z = _mm(glr, wgu_ref[...]) + bg_ref[...]
    log_a = _log_sigmoid(z) * (1.0 / GLA_TAU)

    rg = _mm(u, w_ref[:, OFF_RG:OFF_GQ])
    for h in range(RET_HEADS):
        o = o_ret[h]
        g = rg[:, h * RET_V:(h + 1) * RET_V]
        mu = jnp.mean(o, axis=-1, keepdims=True)
        oc = o - mu
        var = jnp.mean(oc * oc, axis=-1, keepdims=True)
        on = oc * lax.rsqrt(var + EPS) * rgain_ref[:, h * RET_V:(h + 1) * RET_V]
        oret_ref[:, h * RET_V:(h + 1) * RET_V] = (on * (g * _sigmoid(g))).astype(_BF16)

    n_sub = BLOCK // GLA_SUB
    same_sub = (row // GLA_SUB) == (col // GLA_SUB)
    tri = jnp.where(causal & same_sub, 1.0, 0.0).astype(_BF16)
    b_all = _cumsum_rows(log_a, tri)
    gg = _mm(u, w_ref[:, OFF_GG:W_COLS])
    m_ret = _sigmoid(_mm(u, wm_ref[:, :D_MODEL]))
    br = _mm(oret_ref[...], wbr_ref[...])
    causal_sub = causal[:GLA_SUB, :GLA_SUB]
    mid = GLA_SUB // 2 - 1
    ln_scale = math.log(GLA_K ** -0.5)
    q_dec = [[None] * n_sub for _ in range(GLA_HEADS)]
    s_in = [[None] * n_sub for _ in range(GLA_HEADS)]
    upd = [[None] * n_sub for _ in range(GLA_HEADS)]
    dcol = [[None] * n_sub for _ in range(GLA_HEADS)]
    for n in range(n_sub):
        r0 = n * GLA_SUB
        for h in range(GLA_HEADS):
            c0 = h * GLA_K
            b = b_all[r0:r0 + GLA_SUB, c0:c0 + GLA_K]
            b_mid = b[mid:mid + 1]
            b_last = b[GLA_SUB - 1:GLA_SUB]
            q_in = gq[r0:r0 + GLA_SUB, c0:c0 + GLA_K] * jnp.exp(b - (b_mid - ln_scale))
            k_in = gk[r0:r0 + GLA_SUB, c0:c0 + GLA_K] * jnp.exp(b_mid - b)
            q_dec[h][n] = (q_in * jnp.exp(b_mid)).astype(_BF16)
            k_end = (k_in * jnp.exp(b_last - b_mid)).astype(_BF16)
            s_in[h][n] = jnp.where(causal_sub, _mm_nt(q_in.astype(_BF16), k_in.astype(_BF16)), 0.0).astype(_BF16)
            upd[h][n] = _mm_tn(k_end, gv[r0:r0 + GLA_SUB, h * GLA_V:(h + 1) * GLA_V])
            dcol[h][n] = jnp.transpose(jnp.exp(b_last))
    o_gla = [[None] * n_sub for _ in range(GLA_HEADS)]
    for h in range(GLA_HEADS):
        state = sgla_ref[h]
        for n in range(n_sub):
            r0 = n * GLA_SUB
            lhs = jnp.concatenate([q_dec[h][n], s_in[h][n]], axis=1)
            rhs = jnp.concatenate([state.astype(_BF16), gv[r0:r0 + GLA_SUB, h * GLA_V:(h + 1) * GLA_V]], axis=0)
            o_gla[h][n] = _mm(lhs, rhs)
            state = state * dcol[h][n] + upd[h][n]
        sgla_ref[h] = state
    m_gla = _sigmoid(_mm(u, wm_ref[:, D_MODEL:]))
    for h in range(GLA_HEADS):
        o = jnp.concatenate(o_gla[h], axis=0)
        g = gg[:, h * GLA_V:(h + 1) * GLA_V]
        on = _rms_norm(o, ggain_ref[:, h * GLA_V:(h + 1) * GLA_V])
        ogla_ref[:, h * GLA_V:(h + 1) * GLA_V] = (on * (g * _sigmoid(g))).astype(_BF16)

    un = _rms_norm(xnxt_ref[...], ng_ref[...]).astype(_BF16)
    cos = cos_ref[...]
    sin = sin_ref[...]
    pos1 = lax.broadcasted_iota(jnp.int32, (BLOCK, 1), 0).astype(_F32) + 1.0

    def rope_heads(t, sign, scale):
        outs = []
        for h in range(RET_HEADS):
            f = jnp.exp(pos1 * (sign * RET_LOG_GAMMA[h])) * scale
            outs.append(_rope(t[:, h * RET_QK:(h + 1) * RET_QK], cos * f, sin * f).astype(_BF16))
        return jnp.concatenate(outs, axis=1)

    qx_n = rope_heads(_mm(un, w_ref[:, OFF_RQ:OFF_RK]), 1.0, 1.0)
    bgl = _mm(ogla_ref[...], wbg_ref[...])
    kz_n = rope_heads(_mm(un, w_ref[:, OFF_RK:OFF_RV]), -1.0, RET_QK ** -0.5)
    merged = m_ret * br + m_gla * bgl
    y = x + _mm(merged.astype(_BF16), wout_ref[...])
    rv_n = _mm(un, w_ref[:, OFF_RV:OFF_RG]).astype(_BF16)
    o_ref[...] = _rms_norm(y, fng_ref[...])
    u_s[...] = un
    qx_s[...] = qx_n
    kz_s[...] = kz_n
    rv_s[...] = rv_n


def _meta_kernel(x_ref, cos_ref, sin_ref, ng_ref, w_ref, wlr_ref, wgu_ref, bg_ref, sret_ref, sgla_ref):
    rows = META_ROWS
    u = _rms_norm(x_ref[...], ng_ref[...]).astype(_BF16)
    cos = cos_ref[...]
    sin = sin_ref[...]
    row = lax.broadcasted_iota(jnp.int32, (rows, rows), 0)
    col = lax.broadcasted_iota(jnp.int32, (rows, rows), 1)
    pos = lax.broadcasted_iota(jnp.int32, (rows, 1), 0).astype(_F32)
    for h in range(RET_HEADS):
        lg = RET_LOG_GAMMA[h]
        k = _rope(_mm(u, w_ref[:, OFF_RK + h * RET_QK:OFF_RK + (h + 1) * RET_QK]), cos, sin)
        k = k * (RET_QK ** -0.5)
        v = _mm(u, w_ref[:, OFF_RV + h * RET_V:OFF_RV + (h + 1) * RET_V]).astype(_BF16)
        zeta = jnp.exp((rows - 1.0 - pos) * lg)
        sret_ref[h] = _mm_tn((k * zeta).astype(_BF16), v)
    log_a = _gla_log_decay(u, wlr_ref, wgu_ref, bg_ref)
    tri = jnp.where(row >= col, 1.0, 0.0).astype(_BF16)
    b_all = _cumsum_rows(log_a, tri)
    for h in range(GLA_HEADS):
        gk = _mm(u, w_ref[:, OFF_GK + h * GLA_K:OFF_GK + (h + 1) * GLA_K])
        gv = _mm(u, w_ref[:, OFF_GV + h * GLA_V:OFF_GV + (h + 1) * GLA_V]).astype(_BF16)
        b = b_all[:, h * GLA_K:(h + 1) * GLA_K]
        k_end = (gk * jnp.exp(b[rows - 1:rows] - b)).astype(_BF16)
        sgla_ref[h] = _mm_tn(k_end, gv)


def _rope_tables(first_pos, count):
    half = RET_QK // 2
    inv = ROPE_BASE ** (-np.arange(half, dtype=np.float64) / half)
    ang = (first_pos + np.arange(count, dtype=np.float64))[:, None] * inv[None, :]
    return jnp.asarray(np.cos(ang), _F32), jnp.asarray(np.sin(ang), _F32)


def _resident(shape):
    return pl.BlockSpec(shape, lambda *_: (0,) * len(shape), pipeline_mode=pl.Buffered(1))


def kernel(x, meta_tokens, norm_gain, w_in, w_gate_up, b_gate, ret_norm_gain, gla_norm_gain,
           w_branch_ret, w_branch_gla, w_out, final_norm_gain):
    bsz, seq, d = x.shape
    assert d == D_MODEL and seq % BLOCK == 0 and w_in.shape[0] == 1
    n_blocks = seq // BLOCK

    assert w_in.shape[2] == SRC_MR + 2 * D_MODEL
    w_main = w_in[0, :, :W_COLS].astype(_BF16)
    w_merge = w_in[0, :, SRC_MR:].astype(_BF16)
    w_lr = jnp.pad(w_in[0, :, SRC_LR:SRC_MR], ((0, 0), (0, RANK_PAD - GLA_RANK))).astype(_BF16)
    wgu =jnp.pad(w_gate_up[0], ((0, RANK_PAD - GLA_RANK), (0, 0))).astype(_BF16)
    bg = b_gate[0].reshape(1, -1)
    ng = norm_gain[0].reshape(1, -1)
    rgain = ret_norm_gain[0].reshape(1, -1)
    ggain = gla_norm_gain[0].reshape(1, -1)
    fng = final_norm_gain.reshape(1, -1)
    wbr = w_branch_ret[0].astype(_BF16)
    wbg = w_branch_gla[0].astype(_BF16)
    wout = w_out[0].astype(_BF16)

    cos_m, sin_m = _rope_tables(N_META - META_ROWS, META_ROWS)
    cos_x, sin_x = _rope_tables(N_META, seq)
    meta_pad = jnp.pad(meta_tokens.astype(_F32), ((META_ROWS - N_META, 0), (0, 0)))

    sret0, sgla0 = pl.pallas_call(
        _meta_kernel,
        out_shape=(jax.ShapeDtypeStruct((RET_HEADS, RET_QK, RET_V), _F32),
                   jax.ShapeDtypeStruct((GLA_HEADS, GLA_K, GLA_V), _F32)),
        compiler_params=pltpu.CompilerParams(vmem_limit_bytes=VMEM_LIMIT_BYTES),
        name="meta_state",
    )(meta_pad, cos_m, sin_m, ng, w_main, w_lr, wgu, bg)

    n_total = bsz * n_blocks
    x2 = x.reshape(bsz * seq, D_MODEL)
    in_specs = [
        pl.BlockSpec((BLOCK, D_MODEL), lambda s: (jnp.maximum(s - 1, 0), 0)),
        pl.BlockSpec((BLOCK, D_MODEL), lambda s: (jnp.minimum(s, n_total - 1), 0)),
        pl.BlockSpec((BLOCK, LANES), lambda s: (lax.rem(jnp.minimum(s, n_total - 1), n_blocks), 0)),
        pl.BlockSpec((BLOCK, LANES), lambda s: (lax.rem(jnp.minimum(s, n_total - 1), n_blocks), 0)),
        _resident((1, D_MODEL)),
        _resident((D_MODEL, W_COLS)),
        _resident((D_MODEL, 2 * D_MODEL)),
        _resident((D_MODEL, RANK_PAD)),
        _resident((RANK_PAD, GLA_HEADS * GLA_K)),
        _resident((1, GLA_HEADS * GLA_K)),
        _resident((1, RET_WIDTH)),
        _resident((1, GLA_WIDTH)),
        _resident((RET_WIDTH, D_MODEL)),
        _resident((GLA_WIDTH, D_MODEL)),
        _resident((D_MODEL, D_MODEL)),
        _resident((1, D_MODEL)),
        _resident((RET_HEADS, RET_QK, RET_V)),
        _resident((GLA_HEADS, GLA_K, GLA_V)),
    ]
    out = pl.pallas_call(
        functools.partial(_main_kernel, n_blocks),
        grid=(n_total + 1,),
        in_specs=in_specs,
        out_specs=pl.BlockSpec((BLOCK, D_MODEL), lambda s: (jnp.maximum(s - 1, 0), 0)),
        out_shape=jax.ShapeDtypeStruct((bsz * seq, D_MODEL), x.dtype),
        scratch_shapes=[
            pltpu.VMEM((RET_HEADS, RET_QK, RET_V), _F32),
            pltpu.VMEM((GLA_HEADS, GLA_K, GLA_V), _F32),
            pltpu.VMEM((BLOCK, RET_WIDTH), _BF16),
            pltpu.VMEM((BLOCK, GLA_WIDTH), _BF16),
            pltpu.VMEM((BLOCK, D_MODEL), _BF16),
            pltpu.VMEM((BLOCK, RET_HEADS * RET_QK), _BF16),
            pltpu.VMEM((BLOCK, RET_HEADS * RET_QK), _BF16),
            pltpu.VMEM((BLOCK, RET_WIDTH), _BF16),
        ],
        compiler_params=pltpu.CompilerParams(
            dimension_semantics=("arbitrary",),
            vmem_limit_bytes=VMEM_LIMIT_BYTES),
        name="hybrid_layer",
    )(x2, x2, cos_x, sin_x, ng, w_main, w_merge, w_lr, wgu, bg, rgain, ggain, wbr, wbg, wout, fng, sret0, sgla0)
    return out.reshape(bsz, seq, D_MODEL)
```

```python
import functools
import math

import numpy as np
import jax
import jax.numpy as jnp
from jax import lax
from jax.experimental import pallas as pl
from jax.experimental.pallas import tpu as pltpu

D_MODEL = 1024
N_META = 16
RET_HEADS = 4
RET_QK = 256
RET_V = 512
RET_WIDTH = RET_HEADS * RET_V
GLA_HEADS = 4
GLA_K = 128
GLA_V = 256
GLA_WIDTH = GLA_HEADS * GLA_V
GLA_RANK = 16
GLA_TAU = 16.0
ROPE_BASE = 10000.0
EPS = 1e-6

LANES = 128
BLOCK = 256
GLA_SUB = 64
META_ROWS = 128
RANK_PAD = LANES
VMEM_LIMIT_BYTES = 60 * 1024 * 1024

OFF_RQ = 0
OFF_RK = OFF_RQ + RET_HEADS * RET_QK
OFF_RV = OFF_RK + RET_HEADS * RET_QK
OFF_RG = OFF_RV + RET_WIDTH
OFF_GQ = OFF_RG + RET_WIDTH
OFF_GK = OFF_GQ + GLA_HEADS * GLA_K
OFF_GV = OFF_GK + GLA_HEADS * GLA_K
OFF_GG = OFF_GV + GLA_WIDTH
W_COLS = OFF_GG + GLA_WIDTH
SRC_LR = W_COLS
SRC_MR = SRC_LR + GLA_RANK

RET_LOG_GAMMA = tuple(math.log1p(-(2.0 ** (-5.0 - h))) for h in range(RET_HEADS))

_F32 = jnp.float32
_BF16 = jnp.bfloat16


def _mm(a, b):
    return jnp.dot(a, b, preferred_element_type=_F32)


def _mm_nt(a, b):
    return lax.dot_general(a, b, (((1,), (1,)), ((), ())), preferred_element_type=_F32)


def _mm_tn(a, b):
    return lax.dot_general(a, b, (((0,), (0,)), ((), ())), preferred_element_type=_F32)


def _sigmoid(x):
    return 0.5 * jnp.tanh(0.5 * x) + 0.5


def _log_sigmoid(x):
    return jnp.minimum(x, 0.0) - jnp.log(1.0 + jnp.exp(-jnp.abs(x)))


def _rms_norm(x, gain):
    return x * lax.rsqrt(jnp.mean(x * x, axis=-1, keepdims=True) + EPS) * gain


def _rope(t, cos, sin):
    t1 = t[:, :LANES]
    t2 = t[:, LANES:]
    return jnp.concatenate([t1 * cos - t2 * sin, t2 * cos + t1 * sin], axis=-1)


def _cumsum_rows(x, tri):
    hi = x.astype(_BF16)
    lo = (x - hi.astype(_F32)).astype(_BF16)
    return _mm(tri, hi) + _mm(tri, lo)


def _gla_log_decay(u, wlr_ref, wgu_ref, bg_ref):
    glr = _mm(u, wlr_ref[...]).astype(_BF16)
    z = _mm(glr, wgu_ref[...]) + bg_ref[...]
    return _log_sigmoid(z) * (1.0 / GLA_TAU)


def _main_kernel(n_seq_blocks, xcur_ref, xnxt_ref, cos_ref, sin_ref, ng_ref, w_ref, wm_ref, wlr_ref, wgu_ref, bg_ref,
                 rgain_ref, ggain_ref, wbr_ref, wbg_ref, wout_ref, fng_ref, sret0_ref, sgla0_ref,
                 o_ref, sret_ref, sgla_ref, oret_ref, ogla_ref, u_s, qx_s, kz_s, rv_s):
    s = pl.program_id(0)
    cur = jnp.maximum(s - 1, 0)

    @pl.when(s == 0)
    def _():
        u_s[...] = jnp.zeros_like(u_s)
        qx_s[...] = jnp.zeros_like(qx_s)
        kz_s[...] = jnp.zeros_like(kz_s)
        rv_s[...] = jnp.zeros_like(rv_s)

    @pl.when(lax.rem(cur, n_seq_blocks) == 0)
    def _():
        sret_ref[...] = sret0_ref[...]
        sgla_ref[...] = sgla0_ref[...]

    x = xcur_ref[...]
    u = u_s[...]
    row = lax.broadcasted_iota(jnp.int32, (BLOCK, BLOCK), 0)
    col = lax.broadcasted_iota(jnp.int32, (BLOCK, BLOCK), 1)
    causal = row >= col

    qx = [qx_s[:, h * RET_QK:(h + 1) * RET_QK] for h in range(RET_HEADS)]
    kz = [kz_s[:, h * RET_QK:(h + 1) * RET_QK] for h in range(RET_HEADS)]
    rv = rv_s[...]
    scores = [jnp.where(causal, _mm_nt(qx[h], kz[h]), 0.0).astype(_BF16) for h in range(RET_HEADS)]
    glr = _mm(u, wlr_ref[...]).astype(_BF16)
    gq = _mm(u, w_ref[:, OFF_GQ:OFF_GK])
    gk = _mm(u, w_ref[:, OFF_GK:OFF_GV])
    o_ret = []
    for h in range(RET_HEADS):
        v = rv[:, h * RET_V:(h + 1) * RET_V]
        lhs = jnp.concatenate([qx[h], scores[h]], axis=1)
        rhs = jnp.concatenate([sret_ref[h].astype(_BF16), v], axis=0)
        o_ret.append(_mm(lhs, rhs))
    for h in range(RET_HEADS):
        v = rv[:, h * RET_V:(h + 1) * RET_V]
        sret_ref[h] = (sret_ref[h] + _mm_tn(kz[h], v)) * math.exp(BLOCK * RET_LOG_GAMMA[h])

    gv = _mm(u, w_ref[:, OFF_GV:OFF_GG]).astype(_BF16)
    z = _mm(glr, wgu_ref[...]) + bg_ref[...]
    log_a = _log_sigmoid(z) * (1.0 / GLA_TAU)

    rg = _mm(u, w_ref[:, OFF_RG:OFF_GQ])
    for h in range(RET_HEADS):
        o = o_ret[h]
        g = rg[:, h * RET_V:(h + 1) * RET_V]
        mu = jnp.mean(o, axis=-1, keepdims=True)
        oc = o - mu
        var = jnp.mean(oc * oc, axis=-1, keepdims=True)
        on = oc * lax.rsqrt(var + EPS) * rgain_ref[:, h * RET_V:(h + 1) * RET_V]
        oret_ref[:, h * RET_V:(h + 1) * RET_V] = (on * (g * _sigmoid(g))).astype(_BF16)

    n_sub = BLOCK // GLA_SUB
    same_sub = (row // GLA_SUB) == (col // GLA_SUB)
    tri = jnp.where(causal & same_sub, 1.0, 0.0).astype(_BF16)
    b_all = _cumsum_rows(log_a, tri)
    gg = _mm(u, w_ref[:, OFF_GG:W_COLS])
    m_ret = _sigmoid(_mm(u, wm_ref[:, :D_MODEL]))
    br = _mm(oret_ref[...], wbr_ref[...])
    causal_sub = causal[:GLA_SUB, :GLA_SUB]
    mid = GLA_SUB // 2 - 1
    ln_scale = math.log(GLA_K ** -0.5)
    q_dec = [[None] * n_sub for _ in range(GLA_HEADS)]
    s_in = [[None] * n_sub for _ in range(GLA_HEADS)]
    upd = [[None] * n_sub for _ in range(GLA_HEADS)]
    dcol = [[None] * n_sub for _ in range(GLA_HEADS)]
    for n in range(n_sub):
        r0 = n * GLA_SUB
        for h in range(GLA_HEADS):
            c0 = h * GLA_K
            b = b_all[r0:r0 + GLA_SUB, c0:c0 + GLA_K]
            b_mid = b[mid:mid + 1]
            b_last = b[GLA_SUB - 1:GLA_SUB]
            q_in = gq[r0:r0 + GLA_SUB, c0:c0 + GLA_K] * jnp.exp(b - (b_mid - ln_scale))
            k_in = gk[r0:r0 + GLA_SUB, c0:c0 + GLA_K] * jnp.exp(b_mid - b)
            q_dec[h][n] = (q_in * jnp.exp(b_mid)).astype(_BF16)
            k_end = (k_in * jnp.exp(b_last - b_mid)).astype(_BF16)
            s_in[h][n] = jnp.where(causal_sub, _mm_nt(q_in.astype(_BF16), k_in.astype(_BF16)), 0.0).astype(_BF16)
            upd[h][n] = _mm_tn(k_end, gv[r0:r0 + GLA_SUB, h * GLA_V:(h + 1) * GLA_V])
            dcol[h][n] = jnp.transpose(jnp.exp(b_last))
    o_gla = [[None] * n_sub for _ in range(GLA_HEADS)]
    for h in range(GLA_HEADS):
        state = sgla_ref[h]
        for n in range(n_sub):
            r0 = n * GLA_SUB
            lhs = jnp.concatenate([q_dec[h][n], s_in[h][n]], axis=1)
            rhs = jnp.concatenate([state.astype(_BF16), gv[r0:r0 + GLA_SUB, h * GLA_V:(h + 1) * GLA_V]], axis=0)
            o_gla[h][n] = _mm(lhs, rhs)
            state = state * dcol[h][n] + upd[h][n]
        sgla_ref[h] = state
    m_gla = _sigmoid(_mm(u, wm_ref[:, D_MODEL:]))
    for h in range(GLA_HEADS):
        o = jnp.concatenate(o_gla[h], axis=0)
        g = gg[:, h * GLA_V:(h + 1) * GLA_V]
        on = _rms_norm(o, ggain_ref[:, h * GLA_V:(h + 1) * GLA_V])
        ogla_ref[:, h * GLA_V:(h + 1) * GLA_V] = (on * (g * _sigmoid(g))).astype(_BF16)

    un = _rms_norm(xnxt_ref[...], ng_ref[...]).astype(_BF16)
    cos = cos_ref[...]
    sin = sin_ref[...]
    pos1 = lax.broadcasted_iota(jnp.int32, (BLOCK, 1), 0).astype(_F32) + 1.0

    def rope_heads(t, sign, scale):
        outs = []
        for h in range(RET_HEADS):
            f = jnp.exp(pos1 * (sign * RET_LOG_GAMMA[h])) * scale
            outs.append(_rope(t[:, h * RET_QK:(h + 1) * RET_QK], cos * f, sin * f).astype(_BF16))
        return jnp.concatenate(outs, axis=1)

    qx_n = rope_heads(_mm(un, w_ref[:, OFF_RQ:OFF_RK]), 1.0, 1.0)
    bgl = _mm(ogla_ref[...], wbg_ref[...])
    kz_n = rope_heads(_mm(un, w_ref[:, OFF_RK:OFF_RV]), -1.0, RET_QK ** -0.5)
    merged = m_ret * br + m_gla * bgl
    y = x + _mm(merged.astype(_BF16), wout_ref[...])
    rv_n = _mm(un, w_ref[:, OFF_RV:OFF_RG]).astype(_BF16)
    o_ref[...] = _rms_norm(y, fng_ref[...])
    u_s[...] = un
    qx_s[...] = qx_n
    kz_s[...] = kz_n
    rv_s[...] = rv_n


def _meta_kernel(x_ref, cos_ref, sin_ref, ng_ref, w_ref, wlr_ref, wgu_ref, bg_ref, sret_ref, sgla_ref):
    rows = META_ROWS
    u = _rms_norm(x_ref[...], ng_ref[...]).astype(_BF16)
    cos = cos_ref[...]
    sin = sin_ref[...]
    row = lax.broadcasted_iota(jnp.int32, (rows, rows), 0)
    col = lax.broadcasted_iota(jnp.int32, (rows, rows), 1)
    pos = lax.broadcasted_iota(jnp.int32, (rows, 1), 0).astype(_F32)
    for h in range(RET_HEADS):
        lg = RET_LOG_GAMMA[h]
        k = _rope(_mm(u, w_ref[:, OFF_RK + h * RET_QK:OFF_RK + (h + 1) * RET_QK]), cos, sin)
        k = k * (RET_QK ** -0.5)
        v = _mm(u, w_ref[:, OFF_RV + h * RET_V:OFF_RV + (h + 1) * RET_V]).astype(_BF16)
        zeta = jnp.exp((rows - 1.0 - pos) * lg)
        sret_ref[h] = _mm_tn((k * zeta).astype(_BF16), v)
    log_a = _gla_log_decay(u, wlr_ref, wgu_ref, bg_ref)
    tri = jnp.where(row >= col, 1.0, 0.0).astype(_BF16)
    b_all = _cumsum_rows(log_a, tri)
    for h in range(GLA_HEADS):
        gk = _mm(u, w_ref[:, OFF_GK + h * GLA_K:OFF_GK + (h + 1) * GLA_K])
        gv = _mm(u, w_ref[:, OFF_GV + h * GLA_V:OFF_GV + (h + 1) * GLA_V]).astype(_BF16)
        b = b_all[:, h * GLA_K:(h + 1) * GLA_K]
        k_end = (gk * jnp.exp(b[rows - 1:rows] - b)).astype(_BF16)
        sgla_ref[h] = _mm_tn(k_end, gv)


def _rope_tables(first_pos, count):
    half = RET_QK // 2
    inv = ROPE_BASE ** (-np.arange(half, dtype=np.float64) / half)
    ang = (first_pos + np.arange(count, dtype=np.float64))[:, None] * inv[None, :]
    return jnp.asarray(np.cos(ang), _F32), jnp.asarray(np.sin(ang), _F32)


def _resident(shape):
    return pl.BlockSpec(shape, lambda *_: (0,) * len(shape), pipeline_mode=pl.Buffered(1))


def kernel(x, meta_tokens, norm_gain, w_in, w_gate_up, b_gate, ret_norm_gain, gla_norm_gain,
           w_branch_ret, w_branch_gla, w_out, final_norm_gain):
    bsz, seq, d = x.shape
    assert d == D_MODEL and seq % BLOCK == 0 and w_in.shape[0] == 1
    n_blocks = seq // BLOCK

    assert w_in.shape[2] == SRC_MR + 2 * D_MODEL
    w_bf = w_in[0].astype(_BF16)
    w_merge = w_bf[:, SRC_MR:]
    w_lr = jnp.pad(w_bf[:, SRC_LR:SRC_MR], ((0, 0), (0, RANK_PAD - GLA_RANK)))
    w_prefix_spec = pl.BlockSpec((D_MODEL, W_COLS), lambda *_: (0, 0), pipeline_mode=pl.Buffered(1))
    wgu =jnp.pad(w_gate_up[0], ((0, RANK_PAD - GLA_RANK), (0, 0))).astype(_BF16)
    bg = b_gate[0].reshape(1, -1)
    ng = norm_gain[0].reshape(1, -1)
    rgain = ret_norm_gain[0].reshape(1, -1)
    ggain = gla_norm_gain[0].reshape(1, -1)
    fng = final_norm_gain.reshape(1, -1)
    wbr = w_branch_ret[0].astype(_BF16)
    wbg = w_branch_gla[0].astype(_BF16)
    wout = w_out[0].astype(_BF16)

    cos_m, sin_m = _rope_tables(N_META - META_ROWS, META_ROWS)
    cos_x, sin_x = _rope_tables(N_META, seq)
    meta_pad = jnp.pad(meta_tokens.astype(_F32), ((META_ROWS - N_META, 0), (0, 0)))

    sret0, sgla0 = pl.pallas_call(
        _meta_kernel,
        out_shape=(jax.ShapeDtypeStruct((RET_HEADS, RET_QK, RET_V), _F32),
                   jax.ShapeDtypeStruct((GLA_HEADS, GLA_K, GLA_V), _F32)),
        grid=(1,),
        in_specs=[
            _resident((META_ROWS, D_MODEL)),
            _resident((META_ROWS, LANES)),
            _resident((META_ROWS, LANES)),
            _resident((1, D_MODEL)),
            w_prefix_spec,
            _resident((D_MODEL, RANK_PAD)),
            _resident((RANK_PAD, GLA_HEADS * GLA_K)),
            _resident((1, GLA_HEADS * GLA_K)),
        ],
        out_specs=(pl.BlockSpec((RET_HEADS, RET_QK, RET_V), lambda i: (0, 0, 0)),
                   pl.BlockSpec((GLA_HEADS, GLA_K, GLA_V), lambda i: (0, 0, 0))),
        compiler_params=pltpu.CompilerParams(
            dimension_semantics=("arbitrary",), vmem_limit_bytes=VMEM_LIMIT_BYTES),
        name="meta_state",
    )(meta_pad, cos_m, sin_m, ng, w_bf, w_lr, wgu, bg)

    n_total = bsz * n_blocks
    x2 = x.reshape(bsz * seq, D_MODEL)
    in_specs = [
        pl.BlockSpec((BLOCK, D_MODEL), lambda s: (jnp.maximum(s - 1, 0), 0)),
        pl.BlockSpec((BLOCK, D_MODEL), lambda s: (jnp.minimum(s, n_total - 1), 0)),
        pl.BlockSpec((BLOCK, LANES), lambda s: (lax.rem(jnp.minimum(s, n_total - 1), n_blocks), 0)),
        pl.BlockSpec((BLOCK, LANES), lambda s: (lax.rem(jnp.minimum(s, n_total - 1), n_blocks), 0)),
        _resident((1, D_MODEL)),
        w_prefix_spec,
        _resident((D_MODEL, 2 * D_MODEL)),
        _resident((D_MODEL, RANK_PAD)),
        _resident((RANK_PAD, GLA_HEADS * GLA_K)),
        _resident((1, GLA_HEADS * GLA_K)),
        _resident((1, RET_WIDTH)),
        _resident((1, GLA_WIDTH)),
        _resident((RET_WIDTH, D_MODEL)),
        _resident((GLA_WIDTH, D_MODEL)),
        _resident((D_MODEL, D_MODEL)),
        _resident((1, D_MODEL)),
        _resident((RET_HEADS, RET_QK, RET_V)),
        _resident((GLA_HEADS, GLA_K, GLA_V)),
    ]
    out = pl.pallas_call(
        functools.partial(_main_kernel, n_blocks),
        grid=(n_total + 1,),
        in_specs=in_specs,
        out_specs=pl.BlockSpec((BLOCK, D_MODEL), lambda s: (jnp.maximum(s - 1, 0), 0)),
        out_shape=jax.ShapeDtypeStruct((bsz * seq, D_MODEL), x.dtype),
        scratch_shapes=[
            pltpu.VMEM((RET_HEADS, RET_QK, RET_V), _F32),
            pltpu.VMEM((GLA_HEADS, GLA_K, GLA_V), _F32),
            pltpu.VMEM((BLOCK, RET_WIDTH), _BF16),
            pltpu.VMEM((BLOCK, GLA_WIDTH), _BF16),
            pltpu.VMEM((BLOCK, D_MODEL), _BF16),
            pltpu.VMEM((BLOCK, RET_HEADS * RET_QK), _BF16),
            pltpu.VMEM((BLOCK, RET_HEADS * RET_QK), _BF16),
            pltpu.VMEM((BLOCK, RET_WIDTH), _BF16),
        ],
        compiler_params=pltpu.CompilerParams(
            dimension_semantics=("arbitrary",),
            vmem_limit_bytes=VMEM_LIMIT_BYTES),
        name="hybrid_layer",
    )(x2, x2, cos_x, sin_x, ng, w_bf, w_merge, w_lr, wgu, bg, rgain, ggain, wbr, wbg, wout, fng, sret0, sgla0)
    return out.reshape(bsz, seq, D_MODEL)
```
